```python
import math
import jax, jax.numpy as jnp
from jax import lax
import numpy as np

D_MODEL = 1024
BATCH = 16
SEQ = 2048
DEPTH = 2

HEAD_DIM = 64
SB_HEADS = 8
DIFF_HEADS = 4
DIFF_V_DIM = 2 * HEAD_DIM
SB_WIDTH = SB_HEADS * HEAD_DIM
DIFF_QK_WIDTH = DIFF_HEADS * 2 * HEAD_DIM
DIFF_V_WIDTH = DIFF_HEADS * DIFF_V_DIM
MIX_WIDTH = SB_WIDTH + DIFF_V_WIDTH
IN_WIDTH = 3 * SB_WIDTH + 2 * DIFF_QK_WIDTH + DIFF_V_WIDTH
CONV_WIDTH = 31
FFN_HIDDEN = -(-(8 * D_MODEL) // (3 * 256)) * 256
Q_BLOCK = 128
N_EVEN = (DEPTH + 1) // 2
N_ODD = DEPTH // 2
DEEPNORM_ALPHA = (2 * DEPTH) ** 0.25
DEEPNORM_BETA = (8 * DEPTH) ** -0.25
LN_EPS = 1e-5
ALIBI_SLOPES = np.array([2.0 ** (-8.0 * (h + 1) / DIFF_HEADS) for h in range(DIFF_HEADS)], dtype=np.float32)

kernel_name = "stickbreak_diffattn_conformer_hybrid"


def layer_norm(x, g, b):
    xf = x.astype(jnp.float32)
    mu = jnp.mean(xf, axis=-1, keepdims=True)
    var = jnp.mean(jnp.square(xf - mu), axis=-1, keepdims=True)
    return ((xf - mu) * lax.rsqrt(var + LN_EPS)).astype(x.dtype) * g + b


def rms_norm(x, g):
    xf = x.astype(jnp.float32)
    ms = jnp.mean(jnp.square(xf), axis=-1, keepdims=True)
    return (xf * lax.rsqrt(ms + LN_EPS)).astype(x.dtype) * g


def stick_breaking_block(q, k, v, t0):
    qb, sk = q.shape[1], k.shape[1]
    z = jnp.einsum('bqhd,bkhd->bhqk', q, k).astype(jnp.float32) / math.sqrt(HEAD_DIM)
    tpos = t0 + jnp.arange(qb)
    spos = jnp.arange(sk)
    strict = spos[None, :] < tpos[:, None]
    log_1mb = jnp.where(strict, jax.nn.log_sigmoid(-z), 0.0)
    between = lax.cumsum(log_1mb, axis=3, reverse=True) - log_1mb
    a = jnp.where(strict, jnp.exp(jax.nn.log_sigmoid(z) + between), 0.0)
    return jnp.einsum('bhqk,bkhd->bqhd', a.astype(v.dtype), v)


def diff_attention_block(q, k, v, t0, lam):
    qb, sk = q.shape[1], k.shape[1]
    s = jnp.einsum('bqhmd,bkhmd->bmhqk', q, k).astype(jnp.float32) / math.sqrt(HEAD_DIM)
    tpos = t0 + jnp.arange(qb)
    spos = jnp.arange(sk)
    dist = (tpos[:, None] - spos[None, :]).astype(jnp.float32)
    bias = -jnp.asarray(ALIBI_SLOPES)[:, None, None] * dist
    causal = spos[None, :] <= tpos[:, None]
    s = jnp.where(causal, s + bias, -jnp.inf)
    p = jax.nn.softmax(s, axis=-1)
    w = p[:, 0] - lam * p[:, 1]
    return jnp.einsum('bhqk,bkhe->bqhe', w.astype(v.dtype), v)


def attention_mixer(h, w_in, w_out, lq1, lk1, lq2, lk2, subln_g, lambda_init):
    bsz, seq, _ = h.shape
    proj = h @ w_in
    o0 = 0
    def take(width):
        nonlocal o0
        part = proj[..., o0:o0 + width]
        o0 += width
        return part
    q_sb = take(SB_WIDTH).reshape(bsz, seq, SB_HEADS, HEAD_DIM)
    k_sb = take(SB_WIDTH).reshape(bsz, seq, SB_HEADS, HEAD_DIM)
    v_sb = take(SB_WIDTH).reshape(bsz, seq, SB_HEADS, HEAD_DIM)
    q_df = take(DIFF_QK_WIDTH).reshape(bsz, seq, DIFF_HEADS, 2, HEAD_DIM)
    k_df = take(DIFF_QK_WIDTH).reshape(bsz, seq, DIFF_HEADS, 2, HEAD_DIM)
    v_df = take(DIFF_V_WIDTH).reshape(bsz, seq, DIFF_HEADS, DIFF_V_DIM)
    f32 = jnp.float32
    lam = (jnp.exp(jnp.sum(lq1.astype(f32) * lk1.astype(f32)))
           - jnp.exp(jnp.sum(lq2.astype(f32) * lk2.astype(f32))) + lambda_init)
    sb_out, df_out = [], []
    for blk in range(seq // Q_BLOCK):
        t0 = blk * Q_BLOCK
        t1 = t0 + Q_BLOCK
        sb_out.append(stick_breaking_block(q_sb[:, t0:t1], k_sb[:, :t1], v_sb[:, :t1], t0))
        df_out.append(diff_attention_block(q_df[:, t0:t1], k_df[:, :t1], v_df[:, :t1], t0, lam))
    o_sb = jnp.concatenate(sb_out, axis=1).reshape(bsz, seq, SB_WIDTH)
    o_df = rms_norm(jnp.concatenate(df_out, axis=1), subln_g) * (1.0 - lambda_init)
    o_df = o_df.reshape(bsz, seq, DIFF_V_WIDTH)
    return jnp.concatenate([o_sb, o_df], axis=-1) @ w_out


def conv_mixer(h, pw1_w, pw1_b, dw_w, dw_b, cln_g, cln_b, pw2_w, pw2_b):
    u = jax.nn.glu(h @ pw1_w + pw1_b, axis=-1)
    u = lax.conv_general_dilated(
        u, dw_w, window_strides=(1,), padding=[(CONV_WIDTH - 1, 0)],
        dimension_numbers=('NWC', 'WIO', 'NWC'), feature_group_count=D_MODEL) + dw_b
    u = jax.nn.silu(layer_norm(u, cln_g, cln_b))
    return u @ pw2_w + pw2_b


def swiglu(h, wg, wu, wd):
    return (jax.nn.silu(h @ wg) * (h @ wu)) @ wd


def setup_inputs(seed: int = 0) -> dict:
    key = jax.random.key(seed)
    ks = jax.random.split(key, 24)
    nrm = jax.random.normal
    f32 = jnp.float32
    x = nrm(ks[0], (BATCH, SEQ, D_MODEL), f32)
    w_in = nrm(ks[1], (N_EVEN, D_MODEL, IN_WIDTH), f32) * D_MODEL ** -0.5
    col = np.arange(IN_WIDTH)
    v_sb_cols = (col >= 2 * SB_WIDTH) & (col < 3 * SB_WIDTH)
    v_df_cols = col >= 3 * SB_WIDTH + 2 * DIFF_QK_WIDTH
    col_scale = np.where(v_sb_cols | v_df_cols, DEEPNORM_BETA, 1.0).astype(np.float32)
    w_in = w_in * jnp.asarray(col_scale)
    w_out = nrm(ks[2], (N_EVEN, MIX_WIDTH, D_MODEL), f32) * MIX_WIDTH ** -0.5 * DEEPNORM_BETA
    lq1 = nrm(ks[3], (N_EVEN, HEAD_DIM), f32) * 0.1
    lk1 = nrm(ks[4], (N_EVEN, HEAD_DIM), f32) * 0.1
    lq2 = nrm(ks[5], (N_EVEN, HEAD_DIM), f32) * 0.1
    lk2 = nrm(ks[6], (N_EVEN, HEAD_DIM), f32) * 0.1
    subln_g = 1.0 + 0.02 * nrm(ks[7], (N_EVEN, DIFF_V_DIM), f32)
    pw1_w = nrm(ks[8], (N_ODD, D_MODEL, 2 * D_MODEL), f32) * D_MODEL ** -0.5
    pw1_b = 0.02 * nrm(ks[9], (N_ODD, 2 * D_MODEL), f32)
    dw_w = nrm(ks[10], (N_ODD, CONV_WIDTH, 1, D_MODEL), f32) * CONV_WIDTH ** -0.5
    dw_b = 0.02 * nrm(ks[11], (N_ODD, D_MODEL), f32)
    cln_g = 1.0 + 0.02 * nrm(ks[12], (N_ODD, D_MODEL), f32)
    cln_b = 0.02 * nrm(ks[13], (N_ODD, D_MODEL), f32)
    pw2_w = nrm(ks[14], (N_ODD, D_MODEL, D_MODEL), f32) * D_MODEL ** -0.5 * DEEPNORM_BETA
    pw2_b = 0.02 * nrm(ks[15], (N_ODD, D_MODEL), f32)
    mix_ln_g = 1.0 + 0.02 * nrm(ks[16], (DEPTH, D_MODEL), f32)
    mix_ln_b = 0.02 * nrm(ks[17], (DEPTH, D_MODEL), f32)
    ffn_wg = nrm(ks[18], (DEPTH, D_MODEL, FFN_HIDDEN), f32) * D_MODEL ** -0.5 * DEEPNORM_BETA
    ffn_wu = nrm(ks[19], (DEPTH, D_MODEL, FFN_HIDDEN), f32) * D_MODEL ** -0.5 * DEEPNORM_BETA
    ffn_wd = nrm(ks[20], (DEPTH, FFN_HIDDEN, D_MODEL), f32) * FFN_HIDDEN ** -0.5 * DEEPNORM_BETA
    ffn_ln_g = 1.0 + 0.02 * nrm(ks[21], (DEPTH, D_MODEL), f32)
    ffn_ln_b = 0.02 * nrm(ks[22], (DEPTH, D_MODEL), f32)
    return {"x": x, "attn_w_in": w_in, "attn_w_out": w_out,
            "diff_lambda_q1": lq1, "diff_lambda_k1": lk1,
            "diff_lambda_q2": lq2, "diff_lambda_k2": lk2, "diff_subln_g": subln_g,
            "conv_pw1_w": pw1_w, "conv_pw1_b": pw1_b, "conv_dw_w": dw_w, "conv_dw_b": dw_b,
            "conv_ln_g": cln_g, "conv_ln_b": cln_b, "conv_pw2_w": pw2_w, "conv_pw2_b": pw2_b,
            "mix_ln_g": mix_ln_g, "mix_ln_b": mix_ln_b,
            "ffn_w_gate": ffn_wg, "ffn_w_up": ffn_wu, "ffn_w_down": ffn_wd,
            "ffn_ln_g": ffn_ln_g, "ffn_ln_b": ffn_ln_b}


def reference(x, attn_w_in, attn_w_out, diff_lambda_q1, diff_lambda_k1,
              diff_lambda_q2, diff_lambda_k2, diff_subln_g,
              conv_pw1_w, conv_pw1_b, conv_dw_w, conv_dw_b, conv_ln_g, conv_ln_b,
              conv_pw2_w, conv_pw2_b, mix_ln_g, mix_ln_b,
              ffn_w_gate, ffn_w_up, ffn_w_down, ffn_ln_g, ffn_ln_b):
    h = x
    for i in range(DEPTH):
        if i % 2 == 0:
            e = i // 2
            lambda_init = 0.8 - 0.6 * math.exp(-0.3 * i)
            m = attention_mixer(h, attn_w_in[e], attn_w_out[e],
                                diff_lambda_q1[e], diff_lambda_k1[e],
                                diff_lambda_q2[e], diff_lambda_k2[e],
                                diff_subln_g[e], lambda_init)
        else:
            o = i // 2
            m = conv_mixer(h, conv_pw1_w[o], conv_pw1_b[o], conv_dw_w[o], conv_dw_b[o],
                           conv_ln_g[o], conv_ln_b[o], conv_pw2_w[o], conv_pw2_b[o])
        h = layer_norm(DEEPNORM_ALPHA * h + m, mix_ln_g[i], mix_ln_b[i])
        f = swiglu(h, ffn_w_gate[i], ffn_w_up[i], ffn_w_down[i])
        h = layer_norm(DEEPNORM_ALPHA * h + f, ffn_ln_g[i], ffn_ln_b[i])
    return h
```

```python
import functools
import math

import numpy as np
import jax
import jax.numpy as jnp
from jax import lax
from jax.experimental import pallas as pl
from jax.experimental.pallas import tpu as pltpu

F32 = jnp.float32
BF16 = jnp.bfloat16

HEAD_DIM = 64
SB_HEADS = 8
DIFF_HEADS = 4
LANES = 128
SB_GROUPS = SB_HEADS * HEAD_DIM // LANES
CONV_WIDTH = 31
CONV_HALO = 32
LN_EPS = 1e-5
NEG_BIG = -1e30
ALIBI_SLOPES = [2.0 ** (-8.0 * (h + 1) / DIFF_HEADS) for h in range(DIFF_HEADS)]

ATTN_BLOCK = 256
ROW_BLOCK = 512
FFN_CHUNK = 256
CONV_ROWS = 64
VMEM_LIMIT = 56 * 1024 * 1024


def _layer_norm(y, g, b):
    mu = jnp.mean(y, axis=-1, keepdims=True)
    d = y - mu
    var = jnp.mean(d * d, axis=-1, keepdims=True)
    return d * lax.rsqrt(var + LN_EPS) * g + b


def _dot(a, b):
    return jnp.dot(a, b, preferred_element_type=F32)


def _dot_nt(a, b):
    return lax.dot_general(a, b, (((1,), (1,)), ((), ())), preferred_element_type=F32)


def _const_spec(shape):
    nd = len(shape)
    return pl.BlockSpec(shape, lambda *_: (0,) * nd, pipeline_mode=pl.Buffered(1))


def _params(*sem):
    return pltpu.CompilerParams(dimension_semantics=sem, vmem_limit_bytes=VMEM_LIMIT)


def _inproj_body(x_ref, w_ref, o_ref, *, n_chunk):
    xb = x_ref[...].astype(BF16)
    width = o_ref.shape[1] // n_chunk
    for c in range(n_chunk):
        sl = slice(c * width, (c + 1) * width)
        o_ref[:, sl] = _dot(xb, w_ref[:, sl]).astype(o_ref.dtype)


def _inproj(x, w):
    t, d = x.shape
    n = w.shape[1]
    return pl.pallas_call(
        functools.partial(_inproj_body, n_chunk=n // 1024),
        out_shape=jax.ShapeDtypeStruct((t, n), BF16),
        grid=(t // ROW_BLOCK,),
        in_specs=[pl.BlockSpec((ROW_BLOCK, d), lambda i: (i, 0)), _const_spec((d, n))],
        out_specs=pl.BlockSpec((ROW_BLOCK, n), lambda i: (i, 0)),
        compiler_params=_params("parallel"),
        name="inproj",
    )(x, w)


def _sb_body(q_ref, k_ref, v_ref, o_ref, acc_ref, carry_ref):
    tq = tk = ATTN_BLOCK
    qi = pl.program_id(2)
    q = q_ref[0] * jnp.asarray(1.0 / math.sqrt(HEAD_DIM), BF16)
    lane = lax.broadcasted_iota(jnp.int32, (1, LANES), 1)
    head_lanes = (lane < HEAD_DIM, lane >= HEAD_DIM)
    tri = (lax.broadcasted_iota(jnp.int32, (tk, tk), 0)
           > lax.broadcasted_iota(jnp.int32, (tk, tk), 1)).astype(BF16)
    col_minus_row = (lax.broadcasted_iota(jnp.int32, (tq, tk), 1)
                     - lax.broadcasted_iota(jnp.int32, (tq, tk), 0))
    acc_ref[...] = jnp.zeros_like(acc_ref)
    carry_ref[...] = jnp.zeros_like(carry_ref)

    def body(i, _):
        kb = qi - i
        start = pl.multiple_of(kb * tk, tk)
        k = k_ref[0, pl.ds(start, tk), :]
        v = v_ref[0, pl.ds(start, tk), :]
        strict = col_minus_row < i * tq
        acc = acc_ref[...]
        for hd in range(2):
            kh = jnp.where(head_lanes[hd], k, jnp.zeros_like(k))
            vh = jnp.where(head_lanes[hd], v, jnp.zeros_like(v))
            z = _dot_nt(q, kh)
            sp = jnp.log(1.0 + jnp.exp(-jnp.abs(z)))
            log_beta = jnp.minimum(z, 0.0) - sp
            log_1mb = jnp.where(strict, -jnp.maximum(z, 0.0) - sp, 0.0)
            hi = log_1mb.astype(BF16)
            lo = (log_1mb - hi.astype(F32)).astype(BF16)
            between = _dot(hi, tri) + _dot(lo, tri)
            c = carry_ref[hd]
            a = jnp.where(strict, jnp.exp(log_beta + between + c), 0.0)
            carry_ref[hd] = c + jnp.sum(log_1mb, axis=-1, keepdims=True)
            acc = acc + _dot(a.astype(BF16), vh)
        acc_ref[...] = acc
        return 0

    lax.fori_loop(0, qi + 1, body, 0)
    o_ref[0] = acc_ref[...].astype(o_ref.dtype)


def _sb_attention(proj):
    b, s, _ = proj.shape
    t = ATTN_BLOCK
    return pl.pallas_call(
        _sb_body,
        out_shape=jax.ShapeDtypeStruct((b, s, SB_GROUPS * LANES), BF16),
        grid=(b, SB_GROUPS, s // t),
        in_specs=[
            pl.BlockSpec((1, t, LANES), lambda bi, g, qi: (bi, qi, g)),
            pl.BlockSpec((1, s, LANES), lambda bi, g, qi: (bi, 0, SB_GROUPS + g)),
            pl.BlockSpec((1, s, LANES), lambda bi, g, qi: (bi, 0, 2 * SB_GROUPS + g)),
        ],
        out_specs=pl.BlockSpec((1, t, LANES), lambda bi, g, qi: (bi, qi, g)),
        scratch_shapes=[pltpu.VMEM((t, LANES), F32), pltpu.VMEM((2, t, 1), F32)],
        compiler_params=_params("parallel", "parallel", "arbitrary"),
        name="sb_attention",
    )(proj, proj, proj)


def _diff_body(lq1_ref, lk1_ref, lq2_ref, lk2_ref, g_ref, q_ref, k_ref, v_ref, o_ref,
               m_ref, l_ref, acc_ref, *, lambda_init):
    tq = tk = ATTN_BLOCK
    h = pl.program_id(1)
    qi = pl.program_id(2)
    slope = jnp.float32(ALIBI_SLOPES[-1])
    for hh in reversed(range(DIFF_HEADS - 1)):
        slope = jnp.where(h == hh, jnp.float32(ALIBI_SLOPES[hh]), slope)
    q = q_ref[0] * jnp.asarray(1.0 / math.sqrt(HEAD_DIM), BF16)
    lane = lax.broadcasted_iota(jnp.int32, (1, LANES), 1)
    map_lanes = (lane < HEAD_DIM, lane >= HEAD_DIM)
    col_minus_row = (lax.broadcasted_iota(jnp.int32, (tq, tk), 1)
                     - lax.broadcasted_iota(jnp.int32, (tq, tk), 0))
    colf = lax.broadcasted_iota(jnp.int32, (1, tk), 1).astype(F32)
    m_ref[...] = jnp.full_like(m_ref, NEG_BIG)
    l_ref[...] = jnp.zeros_like(l_ref)
    acc_ref[...] = jnp.zeros_like(acc_ref)

    def body(kb, _):
        start = pl.multiple_of(kb * tk, tk)
        k = k_ref[0, pl.ds(start, tk), :]
        v = v_ref[0, pl.ds(start, tk), :]
        causal = col_minus_row <= (qi - kb) * tq
        bias = slope * (colf + ((kb - qi) * tk).astype(F32))
        for mp in range(2):
            km = jnp.where(map_lanes[mp], k, jnp.zeros_like(k))
            sc = jnp.where(causal, _dot_nt(q, km) + bias, NEG_BIG)
            m_old = m_ref[mp]
            m_new = jnp.maximum(m_old, jnp.max(sc, axis=-1, keepdims=True))
            alpha = jnp.exp(m_old - m_new)
            p = jnp.exp(sc - m_new)
            l_ref[mp] = alpha * l_ref[mp] + jnp.sum(p, axis=-1, keepdims=True)
            acc_ref[mp] = alpha * acc_ref[mp] + _dot(p.astype(BF16), v)
            m_ref[mp] = m_new
        return 0

    lax.fori_loop(0, qi + 1, body, 0)
    lam = (jnp.exp(jnp.sum(lq1_ref[...] * lk1_ref[...], axis=-1, keepdims=True))
           - jnp.exp(jnp.sum(lq2_ref[...] * lk2_ref[...], axis=-1, keepdims=True))
           + lambda_init)
    o = acc_ref[0] / l_ref[0] - lam * (acc_ref[1] / l_ref[1])
    ms = jnp.mean(o * o, axis=-1, keepdims=True)
    o = (o * lax.rsqrt(ms + LN_EPS)) * g_ref[...] * (1.0 - lambda_init)
    o_ref[0] = o.astype(o_ref.dtype)


def _diff_attention(proj, lq1, lk1, lq2, lk2, subln_g, lambda_init):
    b, s, _ = proj.shape
    t = ATTN_BLOCK
    q0 = 3 * SB_GROUPS
    k0 = q0 + DIFF_HEADS
    v0 = k0 + DIFF_HEADS
    vec = lambda a: a.reshape(1, -1).astype(F32)
    small = pl.BlockSpec((1, HEAD_DIM), lambda bi, g, qi: (0, 0))
    return pl.pallas_call(
        functools.partial(_diff_body, lambda_init=lambda_init),
        out_shape=jax.ShapeDtypeStruct((b, s, DIFF_HEADS * LANES), BF16),
        grid=(b, DIFF_HEADS, s // t),
        in_specs=[
            small, small, small, small,
            pl.BlockSpec((1, LANES), lambda bi, g, qi: (0, 0)),
            pl.BlockSpec((1, t, LANES), lambda bi, g, qi: (bi, qi, q0 + g)),
            pl.BlockSpec((1, s, LANES), lambda bi, g, qi: (bi, 0, k0 + g)),
            pl.BlockSpec((1, s, LANES), lambda bi, g, qi: (bi, 0, v0 + g)),
        ],
        out_specs=pl.BlockSpec((1, t, LANES), lambda bi, g, qi: (bi, qi, g)),
        scratch_shapes=[pltpu.VMEM((2, t, 1), F32), pltpu.VMEM((2, t, 1), F32),
                        pltpu.VMEM((2, t, LANES), F32)],
        compiler_params=_params("parallel", "parallel", "arbitrary"),
        name="diff_attention",
    )(vec(lq1), vec(lk1), vec(lq2), vec(lk2), vec(subln_g), proj, proj, proj)


def _outproj_body(osb_ref, odf_ref, x_ref, wa_ref, wb_ref, g_ref, b_ref, o_ref, *, alpha):
    m = _dot(osb_ref[...], wa_ref[...]) + _dot(odf_ref[...], wb_ref[...])
    o_ref[...] = _layer_norm(alpha * x_ref[...] + m, g_ref[...], b_ref[...])


def _outproj_ln(o_sb, o_df, x, w_a, w_b, g, b, alpha):
    t, d = x.shape
    row = lambda w: pl.BlockSpec((ROW_BLOCK, w), lambda i: (i, 0))
    return pl.pallas_call(
        functools.partial(_outproj_body, alpha=alpha),
        out_shape=jax.ShapeDtypeStruct((t, d), F32),
        grid=(t // ROW_BLOCK,),
        in_specs=[row(o_sb.shape[1]), row(o_df.shape[1]), row(d),
                  _const_spec(w_a.shape), _const_spec(w_b.shape),
                  _const_spec((1, d)), _const_spec((1, d))],
        out_specs=row(d),
        compiler_params=_params("parallel"),
        name="outproj_ln",
    )(o_sb, o_df, x, w_a, w_b, g.reshape(1, d), b.reshape(1, d))


def _ffn_body(h_ref, wg_ref, wu_ref, wd_ref, g_ref, b_ref, o_ref, *, alpha):
    h = h_ref[...]
    hb = h.astype(BF16)
    hidden = wg_ref.shape[1]
    acc = jnp.zeros(h.shape, F32)
    for c in range(hidden // FFN_CHUNK):
        sl = slice(c * FFN_CHUNK, (c + 1) * FFN_CHUNK)
        gate = _dot(hb, wg_ref[:, sl])
        up = _dot(hb, wu_ref[:, sl])
        act = (gate * jax.nn.sigmoid(gate)) * up
        acc = acc + _dot(act.astype(BF16), wd_ref[sl, :])
    o_ref[...] = _layer_norm(alpha * h + acc, g_ref[...], b_ref[...])


def _ffn_ln(h, wg, wu, wd, g, b, alpha):
    t, d = h.shape
    row = pl.BlockSpec((ROW_BLOCK, d), lambda i: (i, 0))
    return pl.pallas_call(
        functools.partial(_ffn_body, alpha=alpha),
        out_shape=jax.ShapeDtypeStruct((t, d), F32),
        grid=(t // ROW_BLOCK,),
        in_specs=[row, _const_spec(wg.shape), _const_spec(wu.shape), _const_spec(wd.shape),
                  _const_spec((1, d)), _const_spec((1, d))],
        out_specs=row,
        compiler_params=_params("parallel"),
        name="ffn_ln",
    )(h, wg, wu, wd, g.reshape(1, d), b.reshape(1, d))


def _conv_body(h_ref, pw1_ref, b1_ref, dww_ref, dwb_ref, cg_ref, cb_ref, pw2_ref, b2_ref,
               g_ref, b_ref, o_ref, u_ref, c_ref, *, alpha):
    tm, d = h_ref.shape[1], h_ref.shape[2]
    h = h_ref[0]
    hb = h.astype(BF16)
    val = _dot(hb, pw1_ref[:, :d]) + b1_ref[:, :d]
    gate = _dot(hb, pw1_ref[:, d:]) + b1_ref[:, d:]

    @pl.when(pl.program_id(1) == 0)
    def _():
        u_ref[0:CONV_HALO, :] = jnp.zeros((CONV_HALO, d), F32)

    u_ref[CONV_HALO:CONV_HALO + tm, :] = val * jax.nn.sigmoid(gate)

    first_tap = CONV_HALO - (CONV_WIDTH - 1)
    for c in range(d // LANES):
        cols = slice(c * LANES, (c + 1) * LANES)

        def rows_body(r, _, cols=cols):
            r0 = pl.multiple_of(r * CONV_ROWS, CONV_ROWS)
            win_rows = CONV_ROWS + CONV_HALO
            win = u_ref[pl.ds(r0, win_rows), cols]
            acc = jnp.broadcast_to(dwb_ref[:, cols], (CONV_ROWS, LANES))
            for rho in range(8):
                shifted = win if rho == 0 else pltpu.roll(win, win_rows - rho, axis=0)
                for j in range(CONV_HALO // 8 + 1):
                    k = 8 * j + rho - first_tap
                    if 0 <= k < CONV_WIDTH:
                        acc = acc + dww_ref[k:k + 1, cols] * shifted[8 * j:8 * j + CONV_ROWS]
            c_ref[pl.ds(r0, CONV_ROWS), cols] = acc
            return 0

        lax.fori_loop(0, tm // CONV_ROWS, rows_body, 0)

    u_ref[0:CONV_HALO, :] = u_ref[tm:tm + CONV_HALO, :]
    t = _layer_norm(c_ref[...], cg_ref[...], cb_ref[...])
    t = t * jax.nn.sigmoid(t)
    y = _dot(t.astype(BF16), pw2_ref[...]) + b2_ref[...]
    o_ref[0] = _layer_norm(alpha * h + y, g_ref[...], b_ref[...])


def _conv_mixer_ln(h, pw1, b1, dww, dwb, cg, cb, pw2, b2, g, b, alpha):
    bsz, s, d = h.shape
    tm = ROW_BLOCK
    vec = lambda a: _const_spec((1, a.shape[-1]))
    r1 = lambda a: a.reshape(1, -1)
    blk = pl.BlockSpec((1, tm, d), lambda bi, si: (bi, si, 0))
    return pl.pallas_call(
        functools.partial(_conv_body, alpha=alpha),
        out_shape=jax.ShapeDtypeStruct((bsz, s, d), F32),
        grid=(bsz, s // tm),
        in_specs=[blk, _const_spec(pw1.shape), vec(b1), _const_spec(dww.shape), vec(dwb),
                  vec(cg), vec(cb), _const_spec(pw2.shape), vec(b2), vec(g), vec(b)],
        out_specs=blk,
        scratch_shapes=[pltpu.VMEM((CONV_HALO + tm, d), F32), pltpu.VMEM((tm, d), F32)],
        compiler_params=_params("parallel", "arbitrary"),
        name="conv_mixer_ln",
    )(h, pw1, r1(b1), dww, r1(dwb), r1(cg), r1(cb), pw2, r1(b2), r1(g), r1(b))


def kernel(x, attn_w_in, attn_w_out, diff_lambda_q1, diff_lambda_k1, diff_lambda_q2, diff_lambda_k2, diff_subln_g, conv_pw1_w, conv_pw1_b, conv_dw_w, conv_dw_b, conv_ln_g, conv_ln_b, conv_pw2_w, conv_pw2_b, mix_ln_g, mix_ln_b, ffn_w_gate, ffn_w_up, ffn_w_down, ffn_ln_g, ffn_ln_b):
    bsz, s, d = x.shape
    depth = mix_ln_g.shape[0]
    alpha = (2 * depth) ** 0.25
    sb_width = SB_HEADS * HEAD_DIM
    h = x.reshape(bsz * s, d)
    for i in range(depth):
        if i % 2 == 0:
            e = i // 2
            lambda_init = 0.8 - 0.6 * math.exp(-0.3 * i)
            proj = _inproj(h, attn_w_in[e].astype(BF16)).reshape(bsz, s, -1)
            o_sb = _sb_attention(proj)
            o_df = _diff_attention(proj, diff_lambda_q1[e], diff_lambda_k1[e],
                                   diff_lambda_q2[e], diff_lambda_k2[e], diff_subln_g[e],
                                   lambda_init)
            w_out = attn_w_out[e].astype(BF16)
            h = _outproj_ln(o_sb.reshape(bsz * s, -1), o_df.reshape(bsz * s, -1), h,
                            w_out[:sb_width], w_out[sb_width:], mix_ln_g[i], mix_ln_b[i], alpha)
        else:
            o = i // 2
            h = _conv_mixer_ln(h.reshape(bsz, s, d), conv_pw1_w[o].astype(BF16), conv_pw1_b[o],
                               conv_dw_w[o].reshape(CONV_WIDTH, d), conv_dw_b[o],
                               conv_ln_g[o], conv_ln_b[o], conv_pw2_w[o].astype(BF16),
                               conv_pw2_b[o], mix_ln_g[i], mix_ln_b[i], alpha).reshape(bsz * s, d)
        h = _ffn_ln(h, ffn_w_gate[i].astype(BF16), ffn_w_up[i].astype(BF16),
                    ffn_w_down[i].astype(BF16), ffn_ln_g[i], ffn_ln_b[i], alpha)
    return h.reshape(bsz, s, d)
```

```python
import functools
import math

import jax
import jax.numpy as jnp
from jax import lax
from jax.experimental import pallas as pl
from jax.experimental.pallas import tpu as pltpu

F32 = jnp.float32
BF16 = jnp.bfloat16

HEAD_DIM = 64
SB_HEADS = 8
DIFF_HEADS = 4
LANES = 128
SB_GROUPS = SB_HEADS * HEAD_DIM // LANES
CONV_WIDTH = 31
CONV_HALO = 32
LN_EPS = 1e-5
NEG_BIG = -1e30
LOG2E = 1.4426950408889634
ALIBI_SLOPES = [2.0 ** (-8.0 * (h + 1) / DIFF_HEADS) for h in range(DIFF_HEADS)]

ATTN_BLOCK = 256
ROW_BLOCK = 512
FFN_CHUNK = 256
CONV_ROWS = 64
VMEM_LIMIT = 56 * 1024 * 1024


def _layer_norm(y, g, b):
    mu = jnp.mean(y, axis=-1, keepdims=True)
    d = y - mu
    var = jnp.mean(d * d, axis=-1, keepdims=True)
    return d * lax.rsqrt(var + LN_EPS) * g + b


def _dot(a, b):
    return jnp.dot(a, b, preferred_element_type=F32)


def _dot_nt(a, b):
    return lax.dot_general(a, b, (((1,), (1,)), ((), ())), preferred_element_type=F32)


def _const_spec(shape):
    nd = len(shape)
    return pl.BlockSpec(shape, lambda *_: (0,) * nd, pipeline_mode=pl.Buffered(1))


def _params(*sem):
    return pltpu.CompilerParams(dimension_semantics=sem, vmem_limit_bytes=VMEM_LIMIT)


def _inproj_body(x_ref, w_ref, o_ref, *, n_chunk):
    xb = x_ref[...].astype(BF16)
    width = o_ref.shape[1] // n_chunk
    for c in range(n_chunk):
        sl = slice(c * width, (c + 1) * width)
        o_ref[:, sl] = _dot(xb, w_ref[:, sl]).astype(o_ref.dtype)


def _inproj(x, w):
    t, d = x.shape
    n = w.shape[1]
    return pl.pallas_call(
        functools.partial(_inproj_body, n_chunk=n // 1024),
        out_shape=jax.ShapeDtypeStruct((t, n), BF16),
        grid=(t // ROW_BLOCK,),
        in_specs=[pl.BlockSpec((ROW_BLOCK, d), lambda i: (i, 0)), _const_spec((d, n))],
        out_specs=pl.BlockSpec((ROW_BLOCK, n), lambda i: (i, 0)),
        compiler_params=_params("parallel"),
        name="inproj",
    )(x, w)


def _split_lanes(x, low_lanes):
    zero = jnp.zeros_like(x)
    return jnp.concatenate([jnp.where(low_lanes, x, zero), jnp.where(low_lanes, zero, x)], axis=0)


def _sb_body(q_ref, k_ref, v_ref, o_ref, qs_ref, tri_ref, acc_ref, carry_ref):
    tq = tk = ATTN_BLOCK
    qi = pl.program_id(1)
    qs_ref[...] = q_ref[0] * jnp.asarray(1.0 / math.sqrt(HEAD_DIM), BF16)
    low_lanes = lax.broadcasted_iota(jnp.int32, (1, LANES), 1) < HEAD_DIM
    row = lax.broadcasted_iota(jnp.int32, (tq, tk), 0)
    col = lax.broadcasted_iota(jnp.int32, (tq, tk), 1)
    tri_ref[...] = (row > col).astype(BF16)
    acc_ref[...] = jnp.zeros_like(acc_ref)
    carry_ref[...] = jnp.zeros_like(carry_ref)

    def tile(kb, diag):
        start = pl.multiple_of(kb * tk, tk)
        strict = col < row
        for g in range(SB_GROUPS):
            cols = slice(g * LANES, (g + 1) * LANES)
            k2 = _split_lanes(k_ref[0, pl.ds(start, tk), cols], low_lanes)
            v2 = _split_lanes(v_ref[0, pl.ds(start, tk), cols], low_lanes)
            z2 = _dot_nt(qs_ref[:, cols], k2)
            a = []
            for hd in range(2):
                z = z2[:, hd * tk:(hd + 1) * tk]
                sp = jnp.log(1.0 + jnp.exp2(jnp.abs(z) * (-LOG2E)))
                lb = jnp.minimum(z, 0.0) - sp
                l0 = lb - z
                if diag:
                    l0 = jnp.where(strict, l0, 0.0)
                hi = l0.astype(BF16)
                lo = (l0 - hi.astype(F32)).astype(BF16)
                w = _dot(jnp.concatenate([hi, lo], axis=0), tri_ref[...])
                between = w[:tq] + w[tq:]
                c = carry_ref[2 * g + hd]
                ah = jnp.exp(lb + between + jnp.concatenate([c] * (tk // LANES), axis=1))
                if diag:
                    ah = jnp.where(strict, ah, 0.0)
                a.append(ah.astype(BF16))
                row_sum = jnp.sum(l0, axis=-1, keepdims=True)
                carry_ref[2 * g + hd] = c + jnp.broadcast_to(row_sum, (tq, LANES))
            acc_ref[:, cols] += _dot(jnp.concatenate(a, axis=1), v2)

    tile(qi, True)

    def body(i, _):
        tile(qi - i, False)
        return 0

    lax.fori_loop(1, qi + 1, body, 0)
    o_ref[0] = acc_ref[...].astype(o_ref.dtype)


def _sb_attention(proj):
    b, s, _ = proj.shape
    t = ATTN_BLOCK
    width = SB_GROUPS * LANES
    return pl.pallas_call(
        _sb_body,
        out_shape=jax.ShapeDtypeStruct((b, s, width), BF16),
        grid=(b, s // t),
        in_specs=[
            pl.BlockSpec((1, t, width), lambda bi, qi: (bi, qi, 0)),
            pl.BlockSpec((1, s, width), lambda bi, qi: (bi, 0, 1)),
            pl.BlockSpec((1, s, width), lambda bi, qi: (bi, 0, 2)),
        ],
        out_specs=pl.BlockSpec((1, t, width), lambda bi, qi: (bi, qi, 0)),
        scratch_shapes=[pltpu.VMEM((t, width), BF16), pltpu.VMEM((t, t), BF16),
                        pltpu.VMEM((t, width), F32), pltpu.VMEM((SB_HEADS, t, LANES), F32)],
        compiler_params=_params("parallel", "arbitrary"),
        name="sb_attention",
    )(proj, proj, proj)


def _diff_body(lq1_ref, lk1_ref, lq2_ref, lk2_ref, g_ref, q_ref, k_ref, v_ref, o_ref,
               qs_ref, m_ref, l_ref, acc_ref, *, lambda_init):
    tq = tk = ATTN_BLOCK
    qi = pl.program_id(1)
    qs_ref[...] = q_ref[0] * jnp.asarray(1.0 / math.sqrt(HEAD_DIM), BF16)
    low_lanes = lax.broadcasted_iota(jnp.int32, (1, LANES), 1) < HEAD_DIM
    row = lax.broadcasted_iota(jnp.int32, (tq, tk), 0)
    col = lax.broadcasted_iota(jnp.int32, (tq, tk), 1)
    colf = lax.broadcasted_iota(jnp.int32, (1, tk), 1).astype(F32)
    m_ref[...] = jnp.full_like(m_ref, NEG_BIG)
    l_ref[...] = jnp.zeros_like(l_ref)
    acc_ref[...] = jnp.zeros_like(acc_ref)

    def tile(kb, diag):
        start = pl.multiple_of(kb * tk, tk)
        causal = col <= row
        key_off = colf + ((kb - qi) * tk).astype(F32)
        for h in range(DIFF_HEADS):
            cols = slice(h * LANES, (h + 1) * LANES)
            k2 = _split_lanes(k_ref[0, pl.ds(start, tk), cols], low_lanes)
            bias = (ALIBI_SLOPES[h] * LOG2E) * key_off
            s2 = _dot_nt(qs_ref[:, cols], k2) * LOG2E
            probs, alphas = [], []
            for mp in range(2):
                idx = 2 * h + mp
                sc = s2[:, mp * tk:(mp + 1) * tk] + bias
                if diag:
                    sc = jnp.where(causal, sc, NEG_BIG)
                m_old = m_ref[idx]
                row_max = jnp.max(sc, axis=-1, keepdims=True)
                m_new = jnp.maximum(m_old, jnp.broadcast_to(row_max, (tq, LANES)))
                alpha = jnp.exp2(m_old - m_new)
                p = jnp.exp2(sc - jnp.concatenate([m_new] * (tk // LANES), axis=1))
                l_ref[idx] = alpha * l_ref[idx] + sum(
                    p[:, j * LANES:(j + 1) * LANES] for j in range(tk // LANES))
                m_ref[idx] = m_new
                probs.append(p.astype(BF16))
                alphas.append(alpha)
            pv = _dot(jnp.concatenate(probs, axis=0), v_ref[0, pl.ds(start, tk), cols])
            for mp in range(2):
                idx = 2 * h + mp
                acc_ref[idx] = alphas[mp] * acc_ref[idx] + pv[mp * tq:(mp + 1) * tq]

    tile(qi, True)

    def body(i, _):
        tile(qi - i, False)
        return 0

    lax.fori_loop(1, qi + 1, body, 0)
    lam = (jnp.exp(jnp.sum(lq1_ref[...] * lk1_ref[...], axis=-1, keepdims=True))
           - jnp.exp(jnp.sum(lq2_ref[...] * lk2_ref[...], axis=-1, keepdims=True))
           + lambda_init)
    for h in range(DIFF_HEADS):
        l0 = jnp.sum(l_ref[2 * h], axis=-1, keepdims=True)
        l1 = jnp.sum(l_ref[2 * h + 1], axis=-1, keepdims=True)
        o = acc_ref[2 * h] / l0 - lam * (acc_ref[2 * h + 1] / l1)
        ms = jnp.mean(o * o, axis=-1, keepdims=True)
        o = (o * lax.rsqrt(ms + LN_EPS)) * g_ref[...] * (1.0 - lambda_init)
        o_ref[0, :, h * LANES:(h + 1) * LANES] = o.astype(o_ref.dtype)


def _diff_attention(proj, lq1, lk1, lq2, lk2, subln_g, lambda_init):
    b, s, _ = proj.shape
    t = ATTN_BLOCK
    width = DIFF_HEADS * LANES
    vec = lambda a: a.reshape(1, -1).astype(F32)
    small = pl.BlockSpec((1, HEAD_DIM), lambda bi, qi: (0, 0))
    return pl.pallas_call(
        functools.partial(_diff_body, lambda_init=lambda_init),
        out_shape=jax.ShapeDtypeStruct((b, s, width), BF16),
        grid=(b, s // t),
        in_specs=[
            small, small, small, small,
            pl.BlockSpec((1, LANES), lambda bi, qi: (0, 0)),
            pl.BlockSpec((1, t, width), lambda bi, qi: (bi, qi, 3)),
            pl.BlockSpec((1, s, width), lambda bi, qi: (bi, 0, 4)),
            pl.BlockSpec((1, s, width), lambda bi, qi: (bi, 0, 5)),
        ],
        out_specs=pl.BlockSpec((1, t, width), lambda bi, qi: (bi, qi, 0)),
        scratch_shapes=[pltpu.VMEM((t, width), BF16),
                        pltpu.VMEM((2 * DIFF_HEADS, t, LANES), F32),
                        pltpu.VMEM((2 * DIFF_HEADS, t, LANES), F32),
                        pltpu.VMEM((2 * DIFF_HEADS, t, LANES), F32)],
        compiler_params=_params("parallel", "arbitrary"),
        name="diff_attention",
    )(vec(lq1), vec(lk1), vec(lq2), vec(lk2), vec(subln_g), proj, proj, proj)


def _swiglu_ln(h, wg_ref, wu_ref, wd_ref, g, b, alpha):
    hb = h.astype(BF16)
    hidden = wg_ref.shape[1]
    acc = jnp.zeros(h.shape, F32)
    for c in range(hidden // FFN_CHUNK):
        sl = slice(c * FFN_CHUNK, (c + 1) * FFN_CHUNK)
        gate = _dot(hb, wg_ref[:, sl])
        up = _dot(hb, wu_ref[:, sl])
        act = (gate * jax.nn.sigmoid(gate)) * up
        acc = acc + _dot(act.astype(BF16), wd_ref[sl, :])
    return _layer_norm(alpha * h + acc, g, b)


def _outproj_ffn_body(osb_ref, odf_ref, x_ref, wa_ref, wb_ref, mg_ref, mb_ref,
                      wg_ref, wu_ref, wd_ref, fg_ref, fb_ref, o_ref, *, alpha):
    m = _dot(osb_ref[...], wa_ref[...]) + _dot(odf_ref[...], wb_ref[...])
    h = _layer_norm(alpha * x_ref[...] + m, mg_ref[...], mb_ref[...])
    o_ref[...] = _swiglu_ln(h, wg_ref, wu_ref, wd_ref, fg_ref[...], fb_ref[...], alpha)


def _outproj_ffn(o_sb, o_df, x, w_a, w_b, mg, mb, wg, wu, wd, fg, fb, alpha):
    t, d = x.shape
    row = lambda w: pl.BlockSpec((ROW_BLOCK, w), lambda i: (i, 0))
    vec = _const_spec((1, d))
    return pl.pallas_call(
        functools.partial(_outproj_ffn_body, alpha=alpha),
        out_shape=jax.ShapeDtypeStruct((t, d), F32),
        grid=(t // ROW_BLOCK,),
        in_specs=[row(o_sb.shape[1]), row(o_df.shape[1]), row(d),
                  _const_spec(w_a.shape), _const_spec(w_b.shape), vec, vec,
                  _const_spec(wg.shape), _const_spec(wu.shape), _const_spec(wd.shape), vec, vec],
        out_specs=row(d),
        compiler_params=_params("parallel"),
        name="outproj_ffn",
    )(o_sb, o_df, x, w_a, w_b, mg.reshape(1, d), mb.reshape(1, d),
      wg, wu, wd, fg.reshape(1, d), fb.reshape(1, d))


def _ffn_body(h_ref, wg_ref, wu_ref, wd_ref, g_ref, b_ref, o_ref, *, alpha):
    o_ref[...] = _swiglu_ln(h_ref[...], wg_ref, wu_ref, wd_ref, g_ref[...], b_ref[...], alpha)


def _ffn_ln(h, wg, wu, wd, g, b, alpha):
    t, d = h.shape
    row = pl.BlockSpec((ROW_BLOCK, d), lambda i: (i, 0))
    return pl.pallas_call(
        functools.partial(_ffn_body, alpha=alpha),
        out_shape=jax.ShapeDtypeStruct((t, d), F32),
        grid=(t // ROW_BLOCK,),
        in_specs=[row, _const_spec(wg.shape), _const_spec(wu.shape), _const_spec(wd.shape),
                  _const_spec((1, d)), _const_spec((1, d))],
        out_specs=row,
        compiler_params=_params("parallel"),
        name="ffn_ln",
    )(h, wg, wu, wd, g.reshape(1, d), b.reshape(1, d))


def _conv_body(h_ref, pw1_ref, b1_ref, dww_ref, dwb_ref, cg_ref, cb_ref, pw2_ref, b2_ref,
               g_ref, b_ref, o_ref, u_ref, c_ref, *, alpha):
    tm, d = h_ref.shape[1], h_ref.shape[2]
    h = h_ref[0]
    hb = h.astype(BF16)
    val = _dot(hb, pw1_ref[:, :d]) + b1_ref[:, :d]
    gate = _dot(hb, pw1_ref[:, d:]) + b1_ref[:, d:]

    @pl.when(pl.program_id(1) == 0)
    def _():
        u_ref[0:CONV_HALO, :] = jnp.zeros((CONV_HALO, d), F32)

    u_ref[CONV_HALO:CONV_HALO + tm, :] = val * jax.nn.sigmoid(gate)

    first_tap = CONV_HALO - (CONV_WIDTH - 1)
    win_rows = CONV_ROWS + CONV_HALO
    for c in range(d // LANES):
        cols = slice(c * LANES, (c + 1) * LANES)
        for r0 in range(0, tm, CONV_ROWS):
            win = u_ref[r0:r0 + win_rows, cols]
            acc = jnp.broadcast_to(dwb_ref[:, cols], (CONV_ROWS, LANES))
            for rho in range(8):
                shifted = win if rho == 0 else pltpu.roll(win, win_rows - rho, axis=0)
                for j in range(CONV_HALO // 8 + 1):
                    k = 8 * j + rho - first_tap
                    if 0 <= k < CONV_WIDTH:
                        acc = acc + dww_ref[k:k + 1, cols] * shifted[8 * j:8 * j + CONV_ROWS]
            c_ref[r0:r0 + CONV_ROWS, cols] = acc

    u_ref[0:CONV_HALO, :] = u_ref[tm:tm + CONV_HALO, :]
    t = _layer_norm(c_ref[...], cg_ref[...], cb_ref[...])
    t = t * jax.nn.sigmoid(t)
    y = _dot(t.astype(BF16), pw2_ref[...]) + b2_ref[...]
    o_ref[0] = _layer_norm(alpha * h + y, g_ref[...], b_ref[...])


def _conv_mixer_ln(h, pw1, b1, dww, dwb, cg, cb, pw2, b2, g, b, alpha):
    bsz, s, d = h.shape
    tm = ROW_BLOCK
    vec = lambda a: _const_spec((1, a.shape[-1]))
    r1 = lambda a: a.reshape(1, -1)
    blk = pl.BlockSpec((1, tm, d), lambda bi, si: (bi, si, 0))
    return pl.pallas_call(
        functools.partial(_conv_body, alpha=alpha),
        out_shape=jax.ShapeDtypeStruct((bsz, s, d), F32),
        grid=(bsz, s // tm),
        in_specs=[blk, _const_spec(pw1.shape), vec(b1), _const_spec(dww.shape), vec(dwb),
                  vec(cg), vec(cb), _const_spec(pw2.shape), vec(b2), vec(g), vec(b)],
        out_specs=blk,
        scratch_shapes=[pltpu.VMEM((CONV_HALO + tm, d), F32), pltpu.VMEM((tm, d), F32)],
        compiler_params=_params("parallel", "arbitrary"),
        name="conv_mixer_ln",
    )(h, pw1, r1(b1), dww, r1(dwb), r1(cg), r1(cb), pw2, r1(b2), r1(g), r1(b))


def kernel(x, attn_w_in, attn_w_out, diff_lambda_q1, diff_lambda_k1, diff_lambda_q2, diff_lambda_k2, diff_subln_g, conv_pw1_w, conv_pw1_b, conv_dw_w, conv_dw_b, conv_ln_g, conv_ln_b, conv_pw2_w, conv_pw2_b, mix_ln_g, mix_ln_b, ffn_w_gate, ffn_w_up, ffn_w_down, ffn_ln_g, ffn_ln_b):
    bsz, s, d = x.shape
    depth = mix_ln_g.shape[0]
    alpha = (2 * depth) ** 0.25
    sb_width = SB_HEADS * HEAD_DIM
    h = x.reshape(bsz * s, d)
    for i in range(depth):
        ffn = (ffn_w_gate[i].astype(BF16), ffn_w_up[i].astype(BF16), ffn_w_down[i].astype(BF16),
               ffn_ln_g[i], ffn_ln_b[i])
        if i % 2 == 0:
            e = i // 2
            lambda_init = 0.8 - 0.6 * math.exp(-0.3 * i)
            proj = _inproj(h, attn_w_in[e].astype(BF16)).reshape(bsz, s, -1)
            o_sb = _sb_attention(proj)
            o_df = _diff_attention(proj, diff_lambda_q1[e], diff_lambda_k1[e],
                                   diff_lambda_q2[e], diff_lambda_k2[e], diff_subln_g[e],
                                   lambda_init)
            w_out = attn_w_out[e].astype(BF16)
            h = _outproj_ffn(o_sb.reshape(bsz * s, -1), o_df.reshape(bsz * s, -1), h,
                             w_out[:sb_width], w_out[sb_width:], mix_ln_g[i], mix_ln_b[i],
                             *ffn, alpha)
        else:
            o = i // 2
            h = _conv_mixer_ln(h.reshape(bsz, s, d), conv_pw1_w[o].astype(BF16), conv_pw1_b[o],
                               conv_dw_w[o].reshape(CONV_WIDTH, d), conv_dw_b[o],
                               conv_ln_g[o], conv_ln_b[o], conv_pw2_w[o].astype(BF16),
                               conv_pw2_b[o], mix_ln_g[i], mix_ln_b[i], alpha).reshape(bsz * s, d)
            h = _ffn_ln(h, *ffn, alpha)
    return h.reshape(bsz, s, d)
```
